```python
import jax, jax.numpy as jnp
from jax import lax
import numpy as np

D_MODEL = 2048
BATCH = 2
SEQ = 8192
DEPTH = 4

CHUNK = 64
GMLP_BLOCK = 128
D_A = 1024
A_GROUPS = 8
D_B = 1024
B_GROUPS = 8
CONV_B = 3
D_C = 1024
CONV_C = 31
N_BRANCH = 3
D_BR = 1024
D_FF = 5632
EPS = 1e-6
D_IN = 2 * D_A + 3 * D_B + 2 * D_C + N_BRANCH * D_MODEL

kernel_name = "hybrid_gmlp_shortconv_conformer_macaron"


def _split_points():
    widths = [D_A, D_A, D_B, D_B, D_B, D_C, D_C, N_BRANCH * D_MODEL]
    return [int(p) for p in np.cumsum(widths)[:-1]]


def rmsnorm(x, g):
    x32 = x.astype(jnp.float32)
    y = x32 * lax.rsqrt(jnp.mean(x32 * x32, axis=-1, keepdims=True) + EPS)
    return (y * g.astype(jnp.float32)).astype(x.dtype)


def layernorm(x, g, b):
    x32 = x.astype(jnp.float32)
    mu = jnp.mean(x32, axis=-1, keepdims=True)
    xc = x32 - mu
    y = xc * lax.rsqrt(jnp.mean(xc * xc, axis=-1, keepdims=True) + EPS)
    return (y * g.astype(jnp.float32) + b.astype(jnp.float32)).astype(x.dtype)


def swiglu(h, w1, w3, w2):
    return (jax.nn.silu(h @ w1) * (h @ w3)) @ w2


def causal_depthwise_conv(x, w):
    K, C = w.shape
    return lax.conv_general_dilated(
        x, w[:, None, :].astype(x.dtype), window_strides=(1,), padding=[(K - 1, 0)],
        dimension_numbers=("NWC", "WIO", "NWC"), feature_group_count=C)


def gmlp_spatial_gating(u, v, w_s, b_s, ln_g, ln_b):
    bsz, s, _ = v.shape
    v = layernorm(v, ln_g, ln_b)
    vb = v.reshape(bsz, s // GMLP_BLOCK, GMLP_BLOCK, A_GROUPS, D_A // A_GROUPS)
    chunk_id = jnp.arange(GMLP_BLOCK) // CHUNK
    mask = chunk_id[None, :] <= chunk_id[:, None]
    ws = jnp.where(mask[None], w_s, jnp.zeros_like(w_s))
    mixed = jnp.einsum("gij,bnjgc->bnigc", ws, vb) + b_s.T[None, None, :, :, None]
    return u * mixed.reshape(bsz, s, D_A)


def setup_inputs(seed: int = 0) -> dict:
    key = jax.random.key(seed)
    ks = jax.random.split(key, 24)
    f32 = jnp.float32

    def nrm(k, shape, scale):
        return jax.random.normal(k, shape, f32) * scale

    def gain(k, shape):
        return 1.0 + 0.05 * jax.random.normal(k, shape, f32)

    return {
        "x": jax.random.normal(ks[0], (BATCH, SEQ, D_MODEL), f32),
        "ffn1_norm": gain(ks[1], (DEPTH, D_MODEL)),
        "ffn1_w1": nrm(ks[2], (DEPTH, D_MODEL, D_FF), D_MODEL ** -0.5),
        "ffn1_w3": nrm(ks[3], (DEPTH, D_MODEL, D_FF), D_MODEL ** -0.5),
        "ffn1_w2": nrm(ks[4], (DEPTH, D_FF, D_MODEL), D_FF ** -0.5),
        "mix_norm": gain(ks[5], (DEPTH, D_MODEL)),
        "w_in": nrm(ks[6], (DEPTH, D_MODEL, D_IN), D_MODEL ** -0.5),
        "b_gate": nrm(ks[7], (DEPTH, N_BRANCH, D_MODEL), 0.02),
        "gmlp_ln_g": gain(ks[8], (DEPTH, D_A)),
        "gmlp_ln_b": nrm(ks[9], (DEPTH, D_A), 0.02),
        "gmlp_w_s": nrm(ks[10], (DEPTH, A_GROUPS, GMLP_BLOCK, GMLP_BLOCK), GMLP_BLOCK ** -0.5),
        "gmlp_b_s": gain(ks[11], (DEPTH, A_GROUPS, GMLP_BLOCK)),
        "sconv_w": nrm(ks[12], (DEPTH, CONV_B, D_B), CONV_B ** -0.5),
        "conf_conv_w": nrm(ks[13], (DEPTH, CONV_C, D_C), CONV_C ** -0.5),
        "conf_conv_b": nrm(ks[14], (DEPTH, D_C), 0.02),
        "conf_ln_g": gain(ks[15], (DEPTH, D_C)),
        "conf_ln_b": nrm(ks[16], (DEPTH, D_C), 0.02),
        "w_branch": nrm(ks[17], (DEPTH, N_BRANCH, D_BR, D_MODEL), D_BR ** -0.5),
        "w_out": nrm(ks[18], (DEPTH, D_MODEL, D_MODEL), D_MODEL ** -0.5),
        "ffn2_norm": gain(ks[19], (DEPTH, D_MODEL)),
        "ffn2_w1": nrm(ks[20], (DEPTH, D_MODEL, D_FF), D_MODEL ** -0.5),
        "ffn2_w3": nrm(ks[21], (DEPTH, D_MODEL, D_FF), D_MODEL ** -0.5),
        "ffn2_w2": nrm(ks[22], (DEPTH, D_FF, D_MODEL), D_FF ** -0.5),
        "final_norm": gain(ks[23], (D_MODEL,)),
    }


def reference(x, ffn1_norm, ffn1_w1, ffn1_w3, ffn1_w2, mix_norm, w_in, b_gate,
              gmlp_ln_g, gmlp_ln_b, gmlp_w_s, gmlp_b_s, sconv_w, conf_conv_w,
              conf_conv_b, conf_ln_g, conf_ln_b, w_branch, w_out, ffn2_norm,
              ffn2_w1, ffn2_w3, ffn2_w2, final_norm):
    bsz, s, _ = x.shape
    splits = _split_points()
    for l in range(DEPTH):
        x = x + 0.5 * swiglu(rmsnorm(x, ffn1_norm[l]), ffn1_w1[l], ffn1_w3[l], ffn1_w2[l])

        h = rmsnorm(x, mix_norm[l])
        proj = h @ w_in[l]
        u, v, b_g, c_g, x_b, glu_val, glu_gate, gates = jnp.split(proj, splits, axis=-1)

        y_a = gmlp_spatial_gating(jax.nn.gelu(u, approximate=False),
                                  jax.nn.gelu(v, approximate=False),
                                  gmlp_w_s[l], gmlp_b_s[l], gmlp_ln_g[l], gmlp_ln_b[l])

        y_b = b_g * causal_depthwise_conv(c_g * x_b, sconv_w[l])

        z = glu_val * jax.nn.sigmoid(glu_gate)
        z = causal_depthwise_conv(z, conf_conv_w[l]) + conf_conv_b[l]
        y_c = jax.nn.silu(layernorm(z, conf_ln_g[l], conf_ln_b[l]))

        gates = jax.nn.sigmoid(gates.reshape(bsz, s, N_BRANCH, D_MODEL) + b_gate[l])
        merged = (gates[:, :, 0] * (y_a @ w_branch[l, 0])
                  + gates[:, :, 1] * (y_b @ w_branch[l, 1])
                  + gates[:, :, 2] * (y_c @ w_branch[l, 2]))
        x = x + merged @ w_out[l]

        x = x + 0.5 * swiglu(rmsnorm(x, ffn2_norm[l]), ffn2_w1[l], ffn2_w3[l], ffn2_w2[l])
    return rmsnorm(x, final_norm)
```

```python
import functools

import jax
import jax.numpy as jnp
from jax import lax
from jax.experimental import pallas as pl
from jax.experimental.pallas import tpu as pltpu

D_MODEL = 2048
CHUNK = 64
GMLP_BLOCK = 128
D_A = 1024
A_GROUPS = 8
D_B = 1024
CONV_B = 3
D_C = 1024
CONV_C = 31
N_BRANCH = 3
D_FF = 5632
EPS = 1e-6
GATE_COL0 = 2 * D_A + 3 * D_B + 2 * D_C

SUBLANES = 8
V7X_VMEM_LIMIT = 56 * 1024 * 1024

TM = 512
TF = 512
TN = 512
HALO_B = SUBLANES
HALO_C = 4 * SUBLANES
CONV_ROWS = 64
CONV_LANES = 256

f32 = jnp.float32
bf16 = jnp.bfloat16


def _rmsnorm(x, g):
    return x * lax.rsqrt(jnp.mean(x * x, axis=-1, keepdims=True) + EPS) * g


def _layernorm(x, g, b):
    mu = jnp.mean(x, axis=-1, keepdims=True)
    xc = x - mu
    return xc * lax.rsqrt(jnp.mean(xc * xc, axis=-1, keepdims=True) + EPS) * g + b


def _dot(a, b):
    return jnp.dot(a, b, preferred_element_type=f32)


def _gelu(x):
    return 0.5 * x * (1.0 + lax.erf(x * (2.0 ** -0.5)))


def _params(*sem):
    return pltpu.CompilerParams(dimension_semantics=sem, vmem_limit_bytes=V7X_VMEM_LIMIT)


def _ffn_kernel(x_ref, g_ref, w1_ref, w3_ref, w2_ref, gf_ref, o_ref, h_ref, *, final):
    j = pl.program_id(1)

    @pl.when(j == 0)
    def _():
        x = x_ref[...]
        h_ref[...] = _rmsnorm(x, g_ref[...]).astype(bf16)
        o_ref[...] = x

    h = h_ref[...]
    a = _dot(h, w1_ref[...])
    b = _dot(h, w3_ref[...])
    act = (0.5 * (a * jax.nn.sigmoid(a)) * b).astype(bf16)
    o_ref[...] += _dot(act, w2_ref[...])

    if final:
        @pl.when(j == pl.num_programs(1) - 1)
        def _():
            o_ref[...] = _rmsnorm(o_ref[...], gf_ref[...])


def _ffn(x, g, w1, w3, w2, gf, l, final):
    m = x.shape[0]
    return pl.pallas_call(
        functools.partial(_ffn_kernel, final=final),
        grid=(m // TM, D_FF // TF),
        in_specs=[
            pl.BlockSpec((TM, D_MODEL), lambda i, j: (i, 0)),
            pl.BlockSpec((None, 1, D_MODEL), lambda i, j: (l, 0, 0)),
            pl.BlockSpec((None, D_MODEL, TF), lambda i, j: (l, 0, j)),
            pl.BlockSpec((None, D_MODEL, TF), lambda i, j: (l, 0, j)),
            pl.BlockSpec((None, TF, D_MODEL), lambda i, j: (l, j, 0)),
            pl.BlockSpec((1, D_MODEL), lambda i, j: (0, 0)),
        ],
        out_specs=pl.BlockSpec((TM, D_MODEL), lambda i, j: (i, 0)),
        out_shape=jax.ShapeDtypeStruct((m, D_MODEL), f32),
        scratch_shapes=[pltpu.VMEM((TM, D_MODEL), bf16)],
        compiler_params=_params("arbitrary", "arbitrary"),
        name="ffn",
    )(x, g, w1, w3, w2, gf)


def _gmlp_kernel(x_ref, g_ref, wu_ref, wv_ref, lng_ref, lnb_ref, ws_ref, bst_ref, o_ref):
    h = _rmsnorm(x_ref[...], g_ref[...]).astype(bf16)
    u = _gelu(_dot(h, wu_ref[...]))
    v = _gelu(_dot(h, wv_ref[...]))
    v = _layernorm(v, lng_ref[...], lnb_ref[...]).astype(bf16)
    row_chunk = lax.broadcasted_iota(jnp.int32, (GMLP_BLOCK, GMLP_BLOCK), 0) // CHUNK
    col_chunk = lax.broadcasted_iota(jnp.int32, (GMLP_BLOCK, GMLP_BLOCK), 1) // CHUNK
    mask = col_chunk <= row_chunk
    gw = D_A // A_GROUPS
    for g in range(A_GROUPS):
        ws = jnp.where(mask, ws_ref[g], 0.0).astype(bf16)
        bias = bst_ref[:, g:g + 1]
        cols = slice(g * gw, (g + 1) * gw)
        for n in range(TM // GMLP_BLOCK):
            rows = slice(n * GMLP_BLOCK, (n + 1) * GMLP_BLOCK)
            mixed = _dot(ws, v[rows, cols]) + bias
            o_ref[rows, cols] = (u[rows, cols] * mixed).astype(bf16)


def _gmlp(x, g, w_in, lng, lnb, ws, bst, l):
    m = x.shape[0]
    wspec = lambda c: pl.BlockSpec((None, D_MODEL, D_A), lambda i: (l, 0, c))
    vec = lambda d: pl.BlockSpec((None, 1, d), lambda i: (l, 0, 0))
    return pl.pallas_call(
        _gmlp_kernel,
        grid=(m // TM,),
        in_specs=[
            pl.BlockSpec((TM, D_MODEL), lambda i: (i, 0)),
            vec(D_MODEL), wspec(0), wspec(1), vec(D_A), vec(D_A),
            pl.BlockSpec((None, A_GROUPS, GMLP_BLOCK, GMLP_BLOCK), lambda i: (l, 0, 0, 0)),
            pl.BlockSpec((None, GMLP_BLOCK, A_GROUPS), lambda i: (l, 0, 0)),
        ],
        out_specs=pl.BlockSpec((TM, D_A), lambda i: (i, 0)),
        out_shape=jax.ShapeDtypeStruct((m, D_A), bf16),
        compiler_params=_params("arbitrary"),
        name="gmlp",
    )(x, g, w_in, w_in, lng, lnb, ws, bst)


def _causal_conv_tile(buf_ref, w_ref, halo, width, rows, lanes):
    acc = None
    for k in range(width):
        start = halo - (width - 1) + k + rows.start
        term = buf_ref[start:start + (rows.stop - rows.start), lanes] * w_ref[k:k + 1, lanes]
        acc = term if acc is None else acc + term
    return acc


def _stash(buf_ref, val, halo, seq_tiles):
    @pl.when(pl.program_id(0) % seq_tiles == 0)
    def _():
        buf_ref[0:halo, :] = jnp.zeros((halo, buf_ref.shape[1]), f32)

    buf_ref[halo:halo + TM, :] = val


def _carry(buf_ref, halo):
    buf_ref[0:halo, :] = buf_ref[TM:TM + halo, :]


def _sconv_kernel(x_ref, g_ref, wb_ref, wc_ref, wx_ref, cw_ref, o_ref, p_ref, b_ref, *, seq_tiles):
    h = _rmsnorm(x_ref[...], g_ref[...]).astype(bf16)
    b_ref[...] = _dot(h, wb_ref[...])
    _stash(p_ref, _dot(h, wc_ref[...]) * _dot(h, wx_ref[...]), HALO_B, seq_tiles)
    for r in range(0, TM, CONV_ROWS):
        for c in range(0, D_B, CONV_LANES):
            rows, lanes = slice(r, r + CONV_ROWS), slice(c, c + CONV_LANES)
            conv = _causal_conv_tile(p_ref, cw_ref, HALO_B, CONV_B, rows, lanes)
            o_ref[rows, lanes] = (b_ref[rows, lanes] * conv).astype(bf16)
    _carry(p_ref, HALO_B)


def _sconv(x, g, w_in, cw, l, seq_tiles):
    m = x.shape[0]
    wspec = lambda c: pl.BlockSpec((None, D_MODEL, D_B), lambda i: (l, 0, c))
    return pl.pallas_call(
        functools.partial(_sconv_kernel, seq_tiles=seq_tiles),
        grid=(m // TM,),
        in_specs=[
            pl.BlockSpec((TM, D_MODEL), lambda i: (i, 0)),
            pl.BlockSpec((None, 1, D_MODEL), lambda i: (l, 0, 0)),
            wspec(2), wspec(3), wspec(4),
            pl.BlockSpec((None, CONV_B, D_B), lambda i: (l, 0, 0)),
        ],
        out_specs=pl.BlockSpec((TM, D_B), lambda i: (i, 0)),
        out_shape=jax.ShapeDtypeStruct((m, D_B), bf16),
        scratch_shapes=[pltpu.VMEM((HALO_B + TM, D_B), f32), pltpu.VMEM((TM, D_B), f32)],
        compiler_params=_params("arbitrary"),
        name="sconv",
    )(x, g, w_in, w_in, w_in, cw)


def _conf_kernel(x_ref, g_ref, wv_ref, wg_ref, cw_ref, cb_ref, lng_ref, lnb_ref, o_ref,
                 z_ref, c_ref, *, seq_tiles):
    h = _rmsnorm(x_ref[...], g_ref[...]).astype(bf16)
    z = _dot(h, wv_ref[...]) * jax.nn.sigmoid(_dot(h, wg_ref[...]))
    _stash(z_ref, z, HALO_C, seq_tiles)
    for r in range(0, TM, CONV_ROWS):
        for c in range(0, D_C, CONV_LANES):
            rows, lanes = slice(r, r + CONV_ROWS), slice(c, c + CONV_LANES)
            c_ref[rows, lanes] = _causal_conv_tile(z_ref, cw_ref, HALO_C, CONV_C, rows, lanes)
    _carry(z_ref, HALO_C)
    y = _layernorm(c_ref[...] + cb_ref[...], lng_ref[...], lnb_ref[...])
    o_ref[...] = (y * jax.nn.sigmoid(y)).astype(bf16)


def _conf(x, g, w_in, cw, cb, lng, lnb, l, seq_tiles):
    m = x.shape[0]
    wspec = lambda c: pl.BlockSpec((None, D_MODEL, D_C), lambda i: (l, 0, c))
    vec = lambda d: pl.BlockSpec((None, 1, d), lambda i: (l, 0, 0))
    return pl.pallas_call(
        functools.partial(_conf_kernel, seq_tiles=seq_tiles),
        grid=(m // TM,),
        in_specs=[
            pl.BlockSpec((TM, D_MODEL), lambda i: (i, 0)),
            vec(D_MODEL), wspec(5), wspec(6),
            pl.BlockSpec((None, CONV_C, D_C), lambda i: (l, 0, 0)),
            vec(D_C), vec(D_C), vec(D_C),
        ],
        out_specs=pl.BlockSpec((TM, D_C), lambda i: (i, 0)),
        out_shape=jax.ShapeDtypeStruct((m, D_C), bf16),
        scratch_shapes=[pltpu.VMEM((HALO_C + TM, D_C), f32), pltpu.VMEM((TM, D_C), f32)],
        compiler_params=_params("arbitrary"),
        name="conf",
    )(x, g, w_in, w_in, cw, cb, lng, lnb)


def _merge_kernel(x_ref, g_ref, ya_ref, yb_ref, yc_ref, wg0_ref, wg1_ref, wg2_ref, bg_ref,
                  wb0_ref, wb1_ref, wb2_ref, wo_ref, o_ref, h_ref):
    n = pl.program_id(1)

    @pl.when(n == 0)
    def _():
        x = x_ref[...]
        h_ref[...] = _rmsnorm(x, g_ref[...]).astype(bf16)
        o_ref[...] = x

    h = h_ref[...]
    merged = None
    branches = ((ya_ref, wg0_ref, wb0_ref), (yb_ref, wg1_ref, wb1_ref), (yc_ref, wg2_ref, wb2_ref))
    for k, (y_ref, wg_ref, wb_ref) in enumerate(branches):
        gate = jax.nn.sigmoid(_dot(h, wg_ref[...]) + bg_ref[k:k + 1, :])
        term = gate * _dot(y_ref[...], wb_ref[...])
        merged = term if merged is None else merged + term
    o_ref[...] += _dot(merged.astype(bf16), wo_ref[...])


def _merge(x, g, ya, yb, yc, w_in, bg, wbr, wo, l):
    m = x.shape[0]
    row = lambda d: pl.BlockSpec((TM, d), lambda i, n: (i, 0))
    gspec = lambda k: pl.BlockSpec(
        (None, D_MODEL, TN), lambda i, n: (l, 0, (GATE_COL0 + k * D_MODEL) // TN + n))
    bspec = lambda k: pl.BlockSpec((None, None, D_A, TN), lambda i, n: (l, k, 0, n))
    return pl.pallas_call(
        _merge_kernel,
        grid=(m // TM, D_MODEL // TN),
        in_specs=[
            row(D_MODEL),
            pl.BlockSpec((None, 1, D_MODEL), lambda i, n: (l, 0, 0)),
            row(D_A), row(D_B), row(D_C),
            gspec(0), gspec(1), gspec(2),
            pl.BlockSpec((None, N_BRANCH, TN), lambda i, n: (l, 0, n)),
            bspec(0), bspec(1), bspec(2),
            pl.BlockSpec((None, TN, D_MODEL), lambda i, n: (l, n, 0)),
        ],
        out_specs=row(D_MODEL),
        out_shape=jax.ShapeDtypeStruct((m, D_MODEL), f32),
        scratch_shapes=[pltpu.VMEM((TM, D_MODEL), bf16)],
        compiler_params=_params("arbitrary", "arbitrary"),
        name="merge",
    )(x, g, ya, yb, yc, w_in, w_in, w_in, bg, wbr, wbr, wbr, wo)


def kernel(x, ffn1_norm, ffn1_w1, ffn1_w3, ffn1_w2, mix_norm, w_in, b_gate, gmlp_ln_g, gmlp_ln_b, gmlp_w_s, gmlp_b_s, sconv_w, conf_conv_w, conf_conv_b, conf_ln_g, conf_ln_b, w_branch, w_out, ffn2_norm, ffn2_w1, ffn2_w3, ffn2_w2, final_norm):
    bsz, s, d = x.shape
    depth = w_in.shape[0]
    assert d == D_MODEL and s % TM == 0
    seq_tiles = s // TM
    vec = lambda p: p[:, None, :]
    cast = lambda w: w.astype(bf16)

    f1 = (cast(ffn1_w1), cast(ffn1_w3), cast(ffn1_w2))
    f2 = (cast(ffn2_w1), cast(ffn2_w3), cast(ffn2_w2))
    w_in_b, w_br_b, w_out_b = cast(w_in), cast(w_branch), cast(w_out)
    ffn1_g, ffn2_g, mix_g = vec(ffn1_norm), vec(ffn2_norm), vec(mix_norm)
    a_lng, a_lnb = vec(gmlp_ln_g), vec(gmlp_ln_b)
    c_b, c_lng, c_lnb = vec(conf_conv_b), vec(conf_ln_g), vec(conf_ln_b)
    bs_t = jnp.swapaxes(gmlp_b_s, 1, 2)
    gf = final_norm[None, :]

    xf = x.reshape(bsz * s, d)
    for l in range(depth):
        xf = _ffn(xf, ffn1_g, *f1, gf, l, False)
        ya = _gmlp(xf, mix_g, w_in_b, a_lng, a_lnb, gmlp_w_s, bs_t, l)
        yb = _sconv(xf, mix_g, w_in_b, sconv_w, l, seq_tiles)
        yc = _conf(xf, mix_g, w_in_b, conf_conv_w, c_b, c_lng, c_lnb, l, seq_tiles)
        xf = _merge(xf, mix_g, ya, yb, yc, w_in_b, b_gate, w_br_b, w_out_b, l)
        xf = _ffn(xf, ffn2_g, *f2, gf, l, l == depth - 1)
    return xf.reshape(bsz, s, d)
```

```python
import functools

import jax
import jax.numpy as jnp
from jax import lax
from jax.experimental import pallas as pl
from jax.experimental.pallas import tpu as pltpu

D_MODEL = 2048
CHUNK = 64
GMLP_BLOCK = 128
D_A = 1024
A_GROUPS = 8
D_B = 1024
CONV_B = 3
D_C = 1024
CONV_C = 31
N_BRANCH = 3
D_BR = 1024
D_FF = 5632
EPS = 1e-6
GATE_COL0 = 2 * D_A + 3 * D_B + 2 * D_C

SUBLANES = 8
V7X_VMEM_LIMIT = 56 * 1024 * 1024

TM_FFN = 1024
TF = 512
TM_MIX = 256
TM_MERGE = 512
TN = 512
HALO_B = SUBLANES
HALO_C = 4 * SUBLANES
SHIFT_ROWS = TM_MIX + HALO_C - SUBLANES
CONV_ROWS = 64
CONV_LANES = 256
CONV_HALF = D_C // 2

f32 = jnp.float32
bf16 = jnp.bfloat16


def _rmsnorm(x, g):
    return x * lax.rsqrt(jnp.mean(x * x, axis=-1, keepdims=True) + EPS) * g


def _layernorm(x, g, b):
    mu = jnp.mean(x, axis=-1, keepdims=True)
    xc = x - mu
    return xc * lax.rsqrt(jnp.mean(xc * xc, axis=-1, keepdims=True) + EPS) * g + b


def _dot(a, b):
    return jnp.dot(a, b, preferred_element_type=f32)


def _gelu(x):
    return 0.5 * x * (1.0 + lax.erf(x * (2.0 ** -0.5)))


def _params(*sem):
    return pltpu.CompilerParams(dimension_semantics=sem, vmem_limit_bytes=V7X_VMEM_LIMIT)


def _resident(shape, index_map):
    return pl.BlockSpec(shape, index_map, pipeline_mode=pl.Buffered(1))


def _ffn_kernel(x_ref, g_ref, w1_ref, w3_ref, w2_ref, gf_ref, o_ref, h_ref, *, final):
    j = pl.program_id(1)

    @pl.when(j == 0)
    def _():
        x = x_ref[...]
        h_ref[...] = _rmsnorm(x, g_ref[...]).astype(bf16)
        o_ref[...] = x

    h = h_ref[...]
    a = _dot(h, w1_ref[...])
    b = _dot(h, w3_ref[...])
    act = (0.5 * (a * jax.nn.sigmoid(a)) * b).astype(bf16)
    o_ref[...] += _dot(act, w2_ref[...])

    if final:
        @pl.when(j == pl.num_programs(1) - 1)
        def _():
            o_ref[...] = _rmsnorm(o_ref[...], gf_ref[...])


def _ffn(x, g, w1, w3, w2, gf, l, final):
    m = x.shape[0]
    tm = TM_FFN
    return pl.pallas_call(
        functools.partial(_ffn_kernel, final=final),
        grid=(m // tm, D_FF // TF),
        in_specs=[
            pl.BlockSpec((tm, D_MODEL), lambda i, j: (i, 0)),
            pl.BlockSpec((None, 1, D_MODEL), lambda i, j: (l, 0, 0)),
            pl.BlockSpec((None, D_MODEL, TF), lambda i, j: (l, 0, j)),
            pl.BlockSpec((None, D_MODEL, TF), lambda i, j: (l, 0, j)),
            pl.BlockSpec((None, TF, D_MODEL), lambda i, j: (l, j, 0)),
            pl.BlockSpec((1, D_MODEL), lambda i, j: (0, 0)),
        ],
        out_specs=pl.BlockSpec((tm, D_MODEL), lambda i, j: (i, 0)),
        out_shape=jax.ShapeDtypeStruct((m, D_MODEL), f32),
        scratch_shapes=[pltpu.VMEM((tm, D_MODEL), bf16)],
        compiler_params=_params("arbitrary", "arbitrary"),
        name="ffn",
    )(x, g, w1, w3, w2, gf)


def _gmlp_mixer(h, wu_ref, wv_ref, lng_ref, lnb_ref, ws_ref, bst_ref, o_ref):
    u = _gelu(_dot(h, wu_ref[...]))
    v = _gelu(_dot(h, wv_ref[...]))
    v = _layernorm(v, lng_ref[...], lnb_ref[...]).astype(bf16)
    row_chunk = lax.broadcasted_iota(jnp.int32, (GMLP_BLOCK, GMLP_BLOCK), 0) // CHUNK
    col_chunk = lax.broadcasted_iota(jnp.int32, (GMLP_BLOCK, GMLP_BLOCK), 1) // CHUNK
    mask = col_chunk <= row_chunk
    gw = D_A // A_GROUPS
    blocks = [slice(n * GMLP_BLOCK, (n + 1) * GMLP_BLOCK) for n in range(TM_MIX // GMLP_BLOCK)]
    for g in range(A_GROUPS):
        ws = jnp.where(mask, ws_ref[g], 0.0).astype(bf16)
        bias = bst_ref[:, g:g + 1]
        cols = slice(g * gw, (g + 1) * gw)
        mixed = _dot(ws, jnp.concatenate([v[rows, cols] for rows in blocks], axis=1)) + bias
        for n, rows in enumerate(blocks):
            o_ref[rows, cols] = (u[rows, cols] * mixed[:, n * gw:(n + 1) * gw]).astype(bf16)


def _sconv_mixer(h, wb_ref, wc_ref, wx_ref, cw_ref, o_ref, p_ref, b_ref):
    b_ref[...] = _dot(h, wb_ref[...])
    p_ref[HALO_B:HALO_B + TM_MIX, :] = _dot(h, wc_ref[...]) * _dot(h, wx_ref[...])
    for r in range(0, TM_MIX, CONV_ROWS):
        for c in range(0, D_B, CONV_LANES):
            lanes = slice(c, c + CONV_LANES)
            acc = None
            for k in range(CONV_B):
                start = HALO_B - (CONV_B - 1) + k + r
                term = p_ref[start:start + CONV_ROWS, lanes] * cw_ref[k:k + 1, lanes]
                acc = term if acc is None else acc + term
            o_ref[r:r + CONV_ROWS, lanes] = (b_ref[r:r + CONV_ROWS, lanes] * acc).astype(bf16)
    p_ref[0:HALO_B, :] = p_ref[TM_MIX:TM_MIX + HALO_B, :]


def _conf_mixer(h, wv_ref, wg_ref, cw_ref, cb_ref, lng_ref, lnb_ref, o_ref, z_ref, zs_ref, c_ref):
    z_ref[HALO_C:HALO_C + TM_MIX, :] = _dot(h, wv_ref[...]) * jax.nn.sigmoid(_dot(h, wg_ref[...]))
    first = HALO_C - (CONV_C - 1)
    for half in range(0, D_C, CONV_HALF):
        for a in range(1, SUBLANES):
            zs_ref[a - 1] = z_ref[a:a + SHIFT_ROWS, half:half + CONV_HALF]
        for r in range(0, TM_MIX, CONV_ROWS):
            for c in range(0, CONV_HALF, CONV_LANES):
                acc = None
                for k in range(CONV_C):
                    q, a = divmod(first + k, SUBLANES)
                    row0 = q * SUBLANES + r
                    if a == 0:
                        src = z_ref[row0:row0 + CONV_ROWS, half + c:half + c + CONV_LANES]
                    else:
                        src = zs_ref[a - 1, row0:row0 + CONV_ROWS, c:c + CONV_LANES]
                    term = src * cw_ref[k:k + 1, half + c:half + c + CONV_LANES]
                    acc = term if acc is None else acc + term
                c_ref[r:r + CONV_ROWS, half + c:half + c + CONV_LANES] = acc
    z_ref[0:HALO_C, :] = z_ref[TM_MIX:TM_MIX + HALO_C, :]
    y = _layernorm(c_ref[...] + cb_ref[...], lng_ref[...], lnb_ref[...])
    o_ref[...] = (y * jax.nn.sigmoid(y)).astype(bf16)


def _mix_kernel(x_ref, g_ref, wu_ref, wv_ref, wb_ref, wc_ref, wx_ref, wval_ref, wgate_ref,
                a_lng_ref, a_lnb_ref, ws_ref, bst_ref, b_cw_ref,
                c_cw_ref, c_cb_ref, c_lng_ref, c_lnb_ref,
                ya_ref, yb_ref, yc_ref,
                p_ref, b_ref, z_ref, zs_ref, c_ref, *, seq_tiles):
    @pl.when(pl.program_id(0) % seq_tiles == 0)
    def _():
        p_ref[0:HALO_B, :] = jnp.zeros((HALO_B, D_B), f32)
        z_ref[0:HALO_C, :] = jnp.zeros((HALO_C, D_C), f32)

    h = _rmsnorm(x_ref[...], g_ref[...]).astype(bf16)
    _conf_mixer(h, wval_ref, wgate_ref, c_cw_ref, c_cb_ref, c_lng_ref, c_lnb_ref, yc_ref,
                z_ref, zs_ref, c_ref)
    _gmlp_mixer(h, wu_ref, wv_ref, a_lng_ref, a_lnb_ref, ws_ref, bst_ref, ya_ref)
    _sconv_mixer(h, wb_ref, wc_ref, wx_ref, b_cw_ref, yb_ref, p_ref, b_ref)


def _mix(x, g, w_in, a_lng, a_lnb, ws, bst, b_cw, c_cw, c_cb, c_lng, c_lnb, l, seq_tiles):
    m = x.shape[0]
    tm = TM_MIX
    wspec = lambda c: _resident((None, D_MODEL, D_BR), lambda i: (l, 0, c))
    vec = lambda d: pl.BlockSpec((None, 1, d), lambda i: (l, 0, 0))
    out = pl.BlockSpec((tm, D_BR), lambda i: (i, 0))
    return pl.pallas_call(
        functools.partial(_mix_kernel, seq_tiles=seq_tiles),
        grid=(m // tm,),
        in_specs=[
            pl.BlockSpec((tm, D_MODEL), lambda i: (i, 0)),
            vec(D_MODEL),
            wspec(0), wspec(1), wspec(2), wspec(3), wspec(4), wspec(5), wspec(6),
            vec(D_A), vec(D_A),
            pl.BlockSpec((None, A_GROUPS, GMLP_BLOCK, GMLP_BLOCK), lambda i: (l, 0, 0, 0)),
            pl.BlockSpec((None, GMLP_BLOCK, A_GROUPS), lambda i: (l, 0, 0)),
            pl.BlockSpec((None, CONV_B, D_B), lambda i: (l, 0, 0)),
            pl.BlockSpec((None, CONV_C, D_C), lambda i: (l, 0, 0)),
            vec(D_C), vec(D_C), vec(D_C),
        ],
        out_specs=[out, out, out],
        out_shape=[jax.ShapeDtypeStruct((m, D_BR), bf16)] * N_BRANCH,
        scratch_shapes=[
            pltpu.VMEM((HALO_B + tm, D_B), f32),
            pltpu.VMEM((tm, D_B), f32),
            pltpu.VMEM((HALO_C + tm, D_C), f32),
            pltpu.VMEM((SUBLANES - 1, SHIFT_ROWS, CONV_HALF), f32),
            pltpu.VMEM((tm, D_C), f32),
        ],
        compiler_params=_params("arbitrary"),
        name="mix",
    )(x, g, *([w_in] * 7), a_lng, a_lnb, ws, bst, b_cw, c_cw, c_cb, c_lng, c_lnb)


def _merge_kernel(x_ref, g_ref, ya_ref, yb_ref, yc_ref, wg0_ref, wg1_ref, wg2_ref, bg_ref,
                  wb0_ref, wb1_ref, wb2_ref, wo_ref, o_ref, h_ref):
    n = pl.program_id(1)

    @pl.when(n == 0)
    def _():
        x = x_ref[...]
        h_ref[...] = _rmsnorm(x, g_ref[...]).astype(bf16)
        o_ref[...] = x

    h = h_ref[...]
    merged = None
    branches = ((ya_ref, wg0_ref, wb0_ref), (yb_ref, wg1_ref, wb1_ref), (yc_ref, wg2_ref, wb2_ref))
    for k, (y_ref, wg_ref, wb_ref) in enumerate(branches):
        gate = jax.nn.sigmoid(_dot(h, wg_ref[...]) + bg_ref[k:k + 1, :])
        term = gate * _dot(y_ref[...], wb_ref[...])
        merged = term if merged is None else merged + term
    o_ref[...] += _dot(merged.astype(bf16), wo_ref[...])


def _merge(x, g, ya, yb, yc, w_in, bg, wbr, wo, l):
    m = x.shape[0]
    tm = TM_MERGE
    row = lambda d: pl.BlockSpec((tm, d), lambda i, n: (i, 0))
    gspec = lambda k: pl.BlockSpec(
        (None, D_MODEL, TN), lambda i, n: (l, 0, (GATE_COL0 + k * D_MODEL) // TN + n))
    bspec = lambda k: pl.BlockSpec((None, None, D_BR, TN), lambda i, n: (l, k, 0, n))
    return pl.pallas_call(
        _merge_kernel,
        grid=(m // tm, D_MODEL // TN),
        in_specs=[
            row(D_MODEL),
            pl.BlockSpec((None, 1, D_MODEL), lambda i, n: (l, 0, 0)),
            row(D_BR), row(D_BR), row(D_BR),
            gspec(0), gspec(1), gspec(2),
            pl.BlockSpec((None, N_BRANCH, TN), lambda i, n: (l, 0, n)),
            bspec(0), bspec(1), bspec(2),
            pl.BlockSpec((None, TN, D_MODEL), lambda i, n: (l, n, 0)),
        ],
        out_specs=row(D_MODEL),
        out_shape=jax.ShapeDtypeStruct((m, D_MODEL), f32),
        scratch_shapes=[pltpu.VMEM((tm, D_MODEL), bf16)],
        compiler_params=_params("arbitrary", "arbitrary"),
        name="merge",
    )(x, g, ya, yb, yc, w_in, w_in, w_in, bg, wbr, wbr, wbr, wo)


def kernel(x, ffn1_norm, ffn1_w1, ffn1_w3, ffn1_w2, mix_norm, w_in, b_gate, gmlp_ln_g, gmlp_ln_b, gmlp_w_s, gmlp_b_s, sconv_w, conf_conv_w, conf_conv_b, conf_ln_g, conf_ln_b, w_branch, w_out, ffn2_norm, ffn2_w1, ffn2_w3, ffn2_w2, final_norm):
    bsz, s, d = x.shape
    depth = w_in.shape[0]
    assert d == D_MODEL and s % TM_FFN == 0 and s % TM_MIX == 0 and s % TM_MERGE == 0
    seq_tiles = s // TM_MIX
    vec = lambda p: p[:, None, :]
    cast = lambda w: w.astype(bf16)

    f1 = (cast(ffn1_w1), cast(ffn1_w3), cast(ffn1_w2))
    f2 = (cast(ffn2_w1), cast(ffn2_w3), cast(ffn2_w2))
    w_in_b, w_br_b, w_out_b = cast(w_in), cast(w_branch), cast(w_out)
    ffn1_g, ffn2_g, mix_g = vec(ffn1_norm), vec(ffn2_norm), vec(mix_norm)
    a_lng, a_lnb = vec(gmlp_ln_g), vec(gmlp_ln_b)
    c_b, c_lng, c_lnb = vec(conf_conv_b), vec(conf_ln_g), vec(conf_ln_b)
    bs_t = jnp.swapaxes(gmlp_b_s, 1, 2)
    gf = final_norm[None, :]

    xf = x.reshape(bsz * s, d)
    for l in range(depth):
        xf = _ffn(xf, ffn1_g, *f1, gf, l, False)
        ya, yb, yc = _mix(xf, mix_g, w_in_b, a_lng, a_lnb, gmlp_w_s, bs_t, sconv_w,
                          conf_conv_w, c_b, c_lng, c_lnb, l, seq_tiles)
        xf = _merge(xf, mix_g, ya, yb, yc, w_in_b, b_gate, w_br_b, w_out_b, l)
        xf = _ffn(xf, ffn2_g, *f2, gf, l, l == depth - 1)
    return xf.reshape(bsz, s, d)
```

```python
import functools

import jax
import jax.numpy as jnp
from jax import lax
from jax.experimental import pallas as pl
from jax.experimental.pallas import tpu as pltpu

D_MODEL = 2048
CHUNK = 64
GMLP_BLOCK = 128
D_A = 1024
A_GROUPS = 8
D_B = 1024
CONV_B = 3
D_C = 1024
CONV_C = 31
N_BRANCH = 3
D_BR = 1024
D_FF = 5632
D_IN = 2 * D_A + 3 * D_B + 2 * D_C + N_BRANCH * D_MODEL
EPS = 1e-6
GATE_COL0 = 2 * D_A + 3 * D_B + 2 * D_C

SUBLANES = 8
V7X_VMEM_LIMIT = 60000 * 1024

TM_FFN = 1024
TF = 512
TM_MIX = 256
TM_MERGE = 512
TN = 512
HALO_B = SUBLANES
HALO_C = 4 * SUBLANES
SHIFT_ROWS = TM_MIX + HALO_C - SUBLANES
CONV_ROWS = 64
CONV_LANES = 256
CONV_HALF = D_C // 2

f32 = jnp.float32
bf16 = jnp.bfloat16


def _rmsnorm(x, g):
    return x * lax.rsqrt(jnp.mean(x * x, axis=-1, keepdims=True) + EPS) * g


def _layernorm(x, g, b):
    mu = jnp.mean(x, axis=-1, keepdims=True)
    xc = x - mu
    return xc * lax.rsqrt(jnp.mean(xc * xc, axis=-1, keepdims=True) + EPS) * g + b


def _dot(a, b):
    return jnp.dot(a, b, preferred_element_type=f32)


def _gelu(x):
    return 0.5 * x * (1.0 + lax.erf(x * (2.0 ** -0.5)))


def _params(*sem):
    return pltpu.CompilerParams(dimension_semantics=sem, vmem_limit_bytes=V7X_VMEM_LIMIT)


def _resident(shape, index_map):
    return pl.BlockSpec(shape, index_map, pipeline_mode=pl.Buffered(1))


def _cast_specs(jobs):
    in_specs, out_specs, out_shapes = [], [], []
    for w, layer, block, index_map in jobs:
        in_specs.append(pl.BlockSpec(
            (None,) + block, lambda *ids, layer=layer, index_map=index_map: (layer,) + index_map(*ids)))
        out_specs.append(pl.BlockSpec(block, index_map))
        out_shapes.append(jax.ShapeDtypeStruct(w.shape[1:], bf16))
    return in_specs, out_specs, out_shapes


def _run_casts(src_refs, dst_refs):
    for src, dst in zip(src_refs, dst_refs):
        dst[...] = src[...].astype(bf16)


def _ffn_kernel(*refs, final, n_cast):
    x_ref, g_ref, w1_ref, w3_ref, w2_ref, gf_ref = refs[:6]
    src_refs = refs[6:6 + n_cast]
    o_ref = refs[6 + n_cast]
    dst_refs = refs[7 + n_cast:7 + 2 * n_cast]
    h_ref = refs[7 + 2 * n_cast]
    j = pl.program_id(1)

    @pl.when(j == 0)
    def _():
        x = x_ref[...]
        h_ref[...] = _rmsnorm(x, g_ref[...]).astype(bf16)
        o_ref[...] = x

    h = h_ref[...]
    a = _dot(h, w1_ref[...])
    b = _dot(h, w3_ref[...])
    act = (0.5 * (a * jax.nn.sigmoid(a)) * b).astype(bf16)
    o_ref[...] += _dot(act, w2_ref[...])
    _run_casts(src_refs, dst_refs)

    if final:
        @pl.when(j == pl.num_programs(1) - 1)
        def _():
            o_ref[...] = _rmsnorm(o_ref[...], gf_ref[...])


def _ffn(x, g, w1, w3, w2, gf, l, final, jobs):
    m = x.shape[0]
    tm = TM_FFN
    cast_in, cast_out, cast_shapes = _cast_specs(jobs)
    return pl.pallas_call(
        functools.partial(_ffn_kernel, final=final, n_cast=len(jobs)),
        grid=(m // tm, D_FF // TF),
        in_specs=[
            pl.BlockSpec((tm, D_MODEL), lambda i, j: (i, 0)),
            pl.BlockSpec((None, 1, D_MODEL), lambda i, j: (l, 0, 0)),
            pl.BlockSpec((D_MODEL, TF), lambda i, j: (0, j)),
            pl.BlockSpec((D_MODEL, TF), lambda i, j: (0, j)),
            pl.BlockSpec((TF, D_MODEL), lambda i, j: (j, 0)),
            pl.BlockSpec((1, D_MODEL), lambda i, j: (0, 0)),
        ] + cast_in,
        out_specs=[pl.BlockSpec((tm, D_MODEL), lambda i, j: (i, 0))] + cast_out,
        out_shape=[jax.ShapeDtypeStruct((m, D_MODEL), f32)] + cast_shapes,
        scratch_shapes=[pltpu.VMEM((tm, D_MODEL), bf16)],
        compiler_params=_params("arbitrary", "arbitrary"),
        name="ffn",
    )(x, g, w1, w3, w2, gf, *[job[0] for job in jobs])


def _gmlp_spatial(u_ref, v_ref, lng_ref, lnb_ref, ws_ref, bst_ref, o_ref):
    v = _layernorm(v_ref[...], lng_ref[...], lnb_ref[...]).astype(bf16)
    row_chunk = lax.broadcasted_iota(jnp.int32, (GMLP_BLOCK, GMLP_BLOCK), 0) // CHUNK
    col_chunk = lax.broadcasted_iota(jnp.int32, (GMLP_BLOCK, GMLP_BLOCK), 1) // CHUNK
    mask = col_chunk <= row_chunk
    gw = D_A // A_GROUPS
    blocks = [slice(n * GMLP_BLOCK, (n + 1) * GMLP_BLOCK) for n in range(TM_MIX // GMLP_BLOCK)]
    for g in range(A_GROUPS):
        ws = jnp.where(mask, ws_ref[g], 0.0).astype(bf16)
        bias = bst_ref[:, g:g + 1]
        cols = slice(g * gw, (g + 1) * gw)
        mixed = _dot(ws, jnp.concatenate([v[rows, cols] for rows in blocks], axis=1)) + bias
        for n, rows in enumerate(blocks):
            o_ref[rows, cols] = (u_ref[rows, cols] * mixed[:, n * gw:(n + 1) * gw]).astype(bf16)


def _sconv(cw_ref, o_ref, p_ref, b_ref):
    for r in range(0, TM_MIX, CONV_ROWS):
        for c in range(0, D_B, CONV_LANES):
            lanes = slice(c, c + CONV_LANES)
            acc = None
            for k in range(CONV_B):
                start = HALO_B - (CONV_B - 1) + k + r
                term = p_ref[start:start + CONV_ROWS, lanes] * cw_ref[k:k + 1, lanes]
                acc = term if acc is None else acc + term
            o_ref[r:r + CONV_ROWS, lanes] = (b_ref[r:r + CONV_ROWS, lanes] * acc).astype(bf16)
    p_ref[0:HALO_B, :] = p_ref[TM_MIX:TM_MIX + HALO_B, :]


def _conf_conv(cw_ref, cb_ref, lng_ref, lnb_ref, o_ref, z_ref, zs_ref, c_ref):
    first = HALO_C - (CONV_C - 1)
    for half in range(0, D_C, CONV_HALF):
        for a in range(1, SUBLANES):
            zs_ref[a - 1] = z_ref[a:a + SHIFT_ROWS, half:half + CONV_HALF]
        for r in range(0, TM_MIX, CONV_ROWS):
            for c in range(0, CONV_HALF, CONV_LANES):
                acc = None
                for k in range(CONV_C):
                    q, a = divmod(first + k, SUBLANES)
                    row0 = q * SUBLANES + r
                    if a == 0:
                        src = z_ref[row0:row0 + CONV_ROWS, half + c:half + c + CONV_LANES]
                    else:
                        src = zs_ref[a - 1, row0:row0 + CONV_ROWS, c:c + CONV_LANES]
                    term = src * cw_ref[k:k + 1, half + c:half + c + CONV_LANES]
                    acc = term if acc is None else acc + term
                c_ref[r:r + CONV_ROWS, half + c:half + c + CONV_LANES] = acc
    z_ref[0:HALO_C, :] = z_ref[TM_MIX:TM_MIX + HALO_C, :]
    y = _layernorm(c_ref[...] + cb_ref[...], lng_ref[...], lnb_ref[...])
    o_ref[...] = (y * jax.nn.sigmoid(y)).astype(bf16)


def _mix_kernel(x_ref, g_ref, wu_ref, wv_ref, wb_ref, wc_ref, wx_ref, wval_ref, wgate_ref,
                a_lng_ref, a_lnb_ref, ws_ref, bst_ref, b_cw_ref,
                c_cw_ref, c_cb_ref, c_lng_ref, c_lnb_ref,
                ya_ref, yb_ref, yc_ref,
                p_ref, b_ref, u_ref, v_ref, z_ref, zs_ref, c_ref, *, seq_tiles):
    @pl.when(pl.program_id(0) % seq_tiles == 0)
    def _():
        p_ref[0:HALO_B, :] = jnp.zeros((HALO_B, D_B), f32)
        z_ref[0:HALO_C, :] = jnp.zeros((HALO_C, D_C), f32)

    h = _rmsnorm(x_ref[...], g_ref[...]).astype(bf16)
    z_ref[HALO_C:HALO_C + TM_MIX, :] = _dot(h, wval_ref[...]) * jax.nn.sigmoid(_dot(h, wgate_ref[...]))
    _conf_conv(c_cw_ref, c_cb_ref, c_lng_ref, c_lnb_ref, yc_ref, z_ref, zs_ref, c_ref)
    u_ref[...] = _gelu(_dot(h, wu_ref[...]))
    v_ref[...] = _gelu(_dot(h, wv_ref[...]))
    _gmlp_spatial(u_ref, v_ref, a_lng_ref, a_lnb_ref, ws_ref, bst_ref, ya_ref)
    b_ref[...] = _dot(h, wb_ref[...])
    p_ref[HALO_B:HALO_B + TM_MIX, :] = _dot(h, wc_ref[...]) * _dot(h, wx_ref[...])
    _sconv(b_cw_ref, yb_ref, p_ref, b_ref)


def _mix(x, g, w_in, a_lng, a_lnb, ws, bst, b_cw, c_cw, c_cb, c_lng, c_lnb, l, seq_tiles):
    m = x.shape[0]
    tm = TM_MIX
    wspec = lambda c: _resident((D_MODEL, D_BR), lambda i: (0, c))
    vec = lambda d: pl.BlockSpec((None, 1, d), lambda i: (l, 0, 0))
    out = pl.BlockSpec((tm, D_BR), lambda i: (i, 0))
    return pl.pallas_call(
        functools.partial(_mix_kernel, seq_tiles=seq_tiles),
        grid=(m // tm,),
        in_specs=[
            pl.BlockSpec((tm, D_MODEL), lambda i: (i, 0)),
            vec(D_MODEL),
            wspec(0), wspec(1), wspec(2), wspec(3), wspec(4), wspec(5), wspec(6),
            vec(D_A), vec(D_A),
            pl.BlockSpec((None, A_GROUPS, GMLP_BLOCK, GMLP_BLOCK), lambda i: (l, 0, 0, 0)),
            pl.BlockSpec((None, GMLP_BLOCK, A_GROUPS), lambda i: (l, 0, 0)),
            pl.BlockSpec((None, CONV_B, D_B), lambda i: (l, 0, 0)),
            pl.BlockSpec((None, CONV_C, D_C), lambda i: (l, 0, 0)),
            vec(D_C), vec(D_C), vec(D_C),
        ],
        out_specs=[out, out, out],
        out_shape=[jax.ShapeDtypeStruct((m, D_BR), bf16)] * N_BRANCH,
        scratch_shapes=[
            pltpu.VMEM((HALO_B + tm, D_B), f32),
            pltpu.VMEM((tm, D_B), f32),
            pltpu.VMEM((tm, D_A), f32),
            pltpu.VMEM((tm, D_A), f32),
            pltpu.VMEM((HALO_C + tm, D_C), f32),
            pltpu.VMEM((SUBLANES - 1, SHIFT_ROWS, CONV_HALF), f32),
            pltpu.VMEM((tm, D_C), f32),
        ],
        compiler_params=_params("arbitrary"),
        name="mix",
    )(x, g, *([w_in] * 7), a_lng, a_lnb, ws, bst, b_cw, c_cw, c_cb, c_lng, c_lnb)


def _merge_kernel(*refs, n_cast):
    (x_ref, g_ref, ya_ref, yb_ref, yc_ref, wg0_ref, wg1_ref, wg2_ref, bg_ref,
     wb0_ref, wb1_ref, wb2_ref, wo_ref) = refs[:13]
    src_refs = refs[13:13 + n_cast]
    o_ref = refs[13 + n_cast]
    dst_refs = refs[14 + n_cast:14 + 2 * n_cast]
    h_ref = refs[14 + 2 * n_cast]
    n = pl.program_id(1)

    @pl.when(n == 0)
    def _():
        x = x_ref[...]
        h_ref[...] = _rmsnorm(x, g_ref[...]).astype(bf16)
        o_ref[...] = x

    h = h_ref[...]
    merged = None
    branches = ((ya_ref, wg0_ref, wb0_ref), (yb_ref, wg1_ref, wb1_ref), (yc_ref, wg2_ref, wb2_ref))
    for k, (y_ref, wg_ref, wb_ref) in enumerate(branches):
        gate = jax.nn.sigmoid(_dot(h, wg_ref[...]) + bg_ref[k:k + 1, :])
        term = gate * _dot(y_ref[...], wb_ref[...])
        merged = term if merged is None else merged + term
    o_ref[...] += _dot(merged.astype(bf16), wo_ref[...])
    _run_casts(src_refs, dst_refs)


def _merge(x, g, ya, yb, yc, w_in, bg, wbr, wo, l, jobs):
    m = x.shape[0]
    tm = TM_MERGE
    row = lambda d: pl.BlockSpec((tm, d), lambda i, n: (i, 0))
    gspec = lambda k: pl.BlockSpec((D_MODEL, TN), lambda i, n: (0, (GATE_COL0 + k * D_MODEL) // TN + n))
    bspec = lambda k: pl.BlockSpec((D_BR, TN), lambda i, n: (k, n))
    cast_in, cast_out, cast_shapes = _cast_specs(jobs)
    return pl.pallas_call(
        functools.partial(_merge_kernel, n_cast=len(jobs)),
        grid=(m // tm, D_MODEL // TN),
        in_specs=[
            row(D_MODEL),
            pl.BlockSpec((None, 1, D_MODEL), lambda i, n: (l, 0, 0)),
            row(D_BR), row(D_BR), row(D_BR),
            gspec(0), gspec(1), gspec(2),
            pl.BlockSpec((None, N_BRANCH, TN), lambda i, n: (l, 0, n)),
            bspec(0), bspec(1), bspec(2),
            pl.BlockSpec((TN, D_MODEL), lambda i, n: (n, 0)),
        ] + cast_in,
        out_specs=[row(D_MODEL)] + cast_out,
        out_shape=[jax.ShapeDtypeStruct((m, D_MODEL), f32)] + cast_shapes,
        scratch_shapes=[pltpu.VMEM((tm, D_MODEL), bf16)],
        compiler_params=_params("arbitrary", "arbitrary"),
        name="merge",
    )(x, g, ya, yb, yc, w_in, w_in, w_in, bg, wbr, wbr, wbr, wo, *[job[0] for job in jobs])


def kernel(x, ffn1_norm, ffn1_w1, ffn1_w3, ffn1_w2, mix_norm, w_in, b_gate, gmlp_ln_g, gmlp_ln_b, gmlp_w_s, gmlp_b_s, sconv_w, conf_conv_w, conf_conv_b, conf_ln_g, conf_ln_b, w_branch, w_out, ffn2_norm, ffn2_w1, ffn2_w3, ffn2_w2, final_norm):
    bsz, s, d = x.shape
    depth = w_in.shape[0]
    m = bsz * s
    assert d == D_MODEL and s % TM_FFN == 0 and s % TM_MIX == 0 and s % TM_MERGE == 0
    seq_tiles = s // TM_MIX
    vec = lambda p: p[:, None, :]

    ffn1_g, ffn2_g, mix_g = vec(ffn1_norm), vec(ffn2_norm), vec(mix_norm)
    a_lng, a_lnb = vec(gmlp_ln_g), vec(gmlp_ln_b)
    c_b, c_lng, c_lnb = vec(conf_conv_b), vec(conf_ln_g), vec(conf_ln_b)
    bs_t = jnp.swapaxes(gmlp_b_s, 1, 2)
    gf = final_norm[None, :]
    w_branch2 = w_branch.reshape(depth, N_BRANCH * D_BR, D_MODEL)

    ni, nj = m // TM_FFN, D_FF // TF
    up_block, up_map = (D_MODEL // ni, TF), lambda i, j: (i, j)
    down_block, down_map = (TF, D_MODEL // ni), lambda i, j: (j, i)
    sq_cols = 8
    sq_map = lambda i, j: (i, jnp.minimum(j, sq_cols - 1))
    br_block = (N_BRANCH * D_BR // ni, D_MODEL // sq_cols)
    out_block = (D_MODEL // ni, D_MODEL // sq_cols)
    mi, mn = m // TM_MERGE, D_MODEL // TN
    mg_map = lambda i, n: (i, n)
    mg_down_block = (D_FF // mi, D_MODEL // mn)
    mg_in_block = (D_MODEL // mi, D_IN // mn)
    assert nj >= sq_cols and D_FF % mi == 0 and D_IN % (mn * 128) == 0

    f1 = tuple(w[0].astype(bf16) for w in (ffn1_w1, ffn1_w3, ffn1_w2))
    w_in_l = w_in[0].astype(bf16)

    xf = x.reshape(m, d)
    for l in range(depth):
        last = l == depth - 1
        jobs = [(ffn2_w1, l, up_block, up_map), (ffn2_w3, l, up_block, up_map),
                (w_branch2, l, br_block, sq_map), (w_out, l, out_block, sq_map)]
        xf, f2_w1, f2_w3, w_br_l, w_out_l = _ffn(xf, ffn1_g, *f1, gf, l, False, jobs)

        ya, yb, yc = _mix(xf, mix_g, w_in_l, a_lng, a_lnb, gmlp_w_s, bs_t, sconv_w,
                          conf_conv_w, c_b, c_lng, c_lnb, l, seq_tiles)

        jobs = [(ffn2_w2, l, mg_down_block, mg_map)]
        if not last:
            jobs.append((w_in, l + 1, mg_in_block, mg_map))
        xf, f2_w2, *nxt = _merge(xf, mix_g, ya, yb, yc, w_in_l, b_gate, w_br_l, w_out_l, l, jobs)
        if not last:
            w_in_l = nxt[0]

        jobs = [] if last else [(ffn1_w1, l + 1, up_block, up_map), (ffn1_w3, l + 1, up_block, up_map),
                                (ffn1_w2, l + 1, down_block, down_map)]
        xf, *f1 = _ffn(xf, ffn2_g, f2_w1, f2_w3, f2_w2, gf, l, last, jobs)
    return xf.reshape(bsz, s, d)
```

```python
import functools

import jax
import jax.numpy as jnp
from jax import lax
from jax.experimental import pallas as pl
from jax.experimental.pallas import tpu as pltpu

D_MODEL = 2048
CHUNK = 64
GMLP_BLOCK = 128
D_A = 1024
A_GROUPS = 8
D_B = 1024
CONV_B = 3
D_C = 1024
CONV_C = 31
N_BRANCH = 3
D_BR = 1024
D_FF = 5632
D_IN = 2 * D_A + 3 * D_B + 2 * D_C + N_BRANCH * D_MODEL
EPS = 1e-6
GATE_COL0 = 2 * D_A + 3 * D_B + 2 * D_C

SUBLANES = 8
V7X_VMEM_LIMIT = 60000 * 1024

TM_FFN = 1024
TF = 512
TM_MIX = 256
TM_MERGE = 512
TN = 512
HALO_B = SUBLANES
HALO_C = 4 * SUBLANES
SHIFT_ROWS = TM_MIX + HALO_C - SUBLANES
CONV_ROWS = 64
CONV_LANES = 256
CONV_HALF = D_C // 2

f32 = jnp.float32
bf16 = jnp.bfloat16


def _rmsnorm(x, g):
    return x * lax.rsqrt(jnp.mean(x * x, axis=-1, keepdims=True) + EPS) * g


def _layernorm(x, g, b):
    mu = jnp.mean(x, axis=-1, keepdims=True)
    xc = x - mu
    return xc * lax.rsqrt(jnp.mean(xc * xc, axis=-1, keepdims=True) + EPS) * g + b


def _dot(a, b):
    return jnp.dot(a, b, preferred_element_type=f32)


def _gelu(x):
    return 0.5 * x * (1.0 + lax.erf(x * (2.0 ** -0.5)))


def _params(*sem):
    return pltpu.CompilerParams(dimension_semantics=sem, vmem_limit_bytes=V7X_VMEM_LIMIT)


def _resident(shape, index_map):
    return pl.BlockSpec(shape, index_map, pipeline_mode=pl.Buffered(1))


def _cast_specs(jobs):
    in_specs, out_specs, out_shapes = [], [], []
    for w, layer, block, index_map in jobs:
        in_specs.append(pl.BlockSpec(
            (None,) + block, lambda *ids, layer=layer, index_map=index_map: (layer,) + index_map(*ids)))
        out_specs.append(pl.BlockSpec(block, index_map))
        out_shapes.append(jax.ShapeDtypeStruct(w.shape[1:], bf16))
    return in_specs, out_specs, out_shapes


def _run_casts(src_refs, dst_refs):
    for src, dst in zip(src_refs, dst_refs):
        dst[...] = src[...].astype(bf16)


def _ffn_kernel(*refs, final, n_cast):
    x_ref, g_ref, w1_ref, w3_ref, w2_ref, gf_ref = refs[:6]
    src_refs = refs[6:6 + n_cast]
    o_ref = refs[6 + n_cast]
    dst_refs = refs[7 + n_cast:7 + 2 * n_cast]
    h_ref = refs[7 + 2 * n_cast]
    j = pl.program_id(1)

    def step(acc_ref):
        h = h_ref[...]
        a = _dot(h, w1_ref[...])
        b = _dot(h, w3_ref[...])
        act = (0.5 * (a * jax.nn.sigmoid(a)) * b).astype(bf16)
        o_ref[...] = acc_ref[...] + _dot(act, w2_ref[...])
        _run_casts(src_refs, dst_refs)

    @pl.when(j == 0)
    def _():
        h_ref[...] = _rmsnorm(x_ref[...], g_ref[...]).astype(bf16)
        step(x_ref)

    @pl.when(j > 0)
    def _():
        step(o_ref)

    if final:
        @pl.when(j == pl.num_programs(1) - 1)
        def _():
            o_ref[...] = _rmsnorm(o_ref[...], gf_ref[...])


def _ffn(x, g, w1, w3, w2, gf, l, final, jobs):
    m = x.shape[0]
    tm = TM_FFN
    cast_in, cast_out, cast_shapes = _cast_specs(jobs)
    return pl.pallas_call(
        functools.partial(_ffn_kernel, final=final, n_cast=len(jobs)),
        grid=(m // tm, D_FF // TF),
        in_specs=[
            pl.BlockSpec((tm, D_MODEL), lambda i, j: (i, 0)),
            pl.BlockSpec((None, 1, D_MODEL), lambda i, j: (l, 0, 0)),
            pl.BlockSpec((D_MODEL, TF), lambda i, j: (0, j)),
            pl.BlockSpec((D_MODEL, TF), lambda i, j: (0, j)),
            pl.BlockSpec((TF, D_MODEL), lambda i, j: (j, 0)),
            pl.BlockSpec((1, D_MODEL), lambda i, j: (0, 0)),
        ] + cast_in,
        out_specs=[pl.BlockSpec((tm, D_MODEL), lambda i, j: (i, 0))] + cast_out,
        out_shape=[jax.ShapeDtypeStruct((m, D_MODEL), f32)] + cast_shapes,
        scratch_shapes=[pltpu.VMEM((tm, D_MODEL), bf16)],
        compiler_params=_params("arbitrary", "arbitrary"),
        name="ffn",
    )(x, g, w1, w3, w2, gf, *[job[0] for job in jobs])


def _gmlp_spatial(u_ref, v_ref, lng_ref, lnb_ref, ws_ref, bst_ref, o_ref):
    v = _layernorm(v_ref[...], lng_ref[...], lnb_ref[...]).astype(bf16)
    row_chunk = lax.broadcasted_iota(jnp.int32, (GMLP_BLOCK, GMLP_BLOCK), 0) // CHUNK
    col_chunk = lax.broadcasted_iota(jnp.int32, (GMLP_BLOCK, GMLP_BLOCK), 1) // CHUNK
    mask = col_chunk <= row_chunk
    gw = D_A // A_GROUPS
    blocks = [slice(n * GMLP_BLOCK, (n + 1) * GMLP_BLOCK) for n in range(TM_MIX // GMLP_BLOCK)]
    for g in range(A_GROUPS):
        ws = jnp.where(mask, ws_ref[g], 0.0).astype(bf16)
        bias = bst_ref[:, g:g + 1]
        cols = slice(g * gw, (g + 1) * gw)
        mixed = _dot(ws, jnp.concatenate([v[rows, cols] for rows in blocks], axis=1)) + bias
        for n, rows in enumerate(blocks):
            o_ref[rows, cols] = (u_ref[rows, cols] * mixed[:, n * gw:(n + 1) * gw]).astype(bf16)


def _sconv(cw_ref, o_ref, p_ref, b_ref):
    for r in range(0, TM_MIX, CONV_ROWS):
        for c in range(0, D_B, CONV_LANES):
            lanes = slice(c, c + CONV_LANES)
            acc = None
            for k in range(CONV_B):
                start = HALO_B - (CONV_B - 1) + k + r
                term = p_ref[start:start + CONV_ROWS, lanes] * cw_ref[k:k + 1, lanes]
                acc = term if acc is None else acc + term
            o_ref[r:r + CONV_ROWS, lanes] = (b_ref[r:r + CONV_ROWS, lanes] * acc).astype(bf16)
    p_ref[0:HALO_B, :] = p_ref[TM_MIX:TM_MIX + HALO_B, :]


def _exact_zero(v):
    bits = lax.bitcast_convert_type(v, jnp.uint32)
    sixteen = jnp.uint32(16)
    return lax.bitcast_convert_type(
        lax.shift_right_logical(lax.shift_right_logical(bits, sixteen), sixteen), f32)


def _conf_conv(cw_ref, cb_ref, lng_ref, lnb_ref, o_ref, z_ref, zs_ref, c_ref, pace):
    tile = 0
    first = HALO_C - (CONV_C - 1)
    for half in range(0, D_C, CONV_HALF):
        for a in range(1, SUBLANES):
            zs_ref[a - 1] = z_ref[a:a + SHIFT_ROWS, half:half + CONV_HALF]
        for r in range(0, TM_MIX, CONV_ROWS):
            for c in range(0, CONV_HALF, CONV_LANES):
                acc = None
                if pace[tile] is not None:
                    col = jnp.concatenate([pace[tile]] * (CONV_ROWS // SUBLANES), axis=0)
                    acc = jnp.concatenate([col] * (CONV_LANES // 128), axis=1)
                tile += 1
                for k in range(CONV_C):
                    q, a = divmod(first + k, SUBLANES)
                    row0 = q * SUBLANES + r
                    if a == 0:
                        src = z_ref[row0:row0 + CONV_ROWS, half + c:half + c + CONV_LANES]
                    else:
                        src = zs_ref[a - 1, row0:row0 + CONV_ROWS, c:c + CONV_LANES]
                    term = src * cw_ref[k:k + 1, half + c:half + c + CONV_LANES]
                    acc = term if acc is None else acc + term
                c_ref[r:r + CONV_ROWS, half + c:half + c + CONV_LANES] = acc
    z_ref[0:HALO_C, :] = z_ref[TM_MIX:TM_MIX + HALO_C, :]
    y = _layernorm(c_ref[...] + cb_ref[...], lng_ref[...], lnb_ref[...])
    o_ref[...] = (y * jax.nn.sigmoid(y)).astype(bf16)


def _mix_kernel(x_ref, g_ref, wu_ref, wv_ref, wb_ref, wc_ref, wx_ref, wval_ref, wgate_ref,
                a_lng_ref, a_lnb_ref, ws_ref, bst_ref, b_cw_ref,
                c_cw_ref, c_cb_ref, c_lng_ref, c_lnb_ref,
                ya_ref, yb_ref, yc_ref,
                p_ref, b_ref, u_ref, v_ref, z_ref, zs_ref, c_ref, *, seq_tiles):
    @pl.when(pl.program_id(0) % seq_tiles == 0)
    def _():
        p_ref[0:HALO_B, :] = jnp.zeros((HALO_B, D_B), f32)
        z_ref[0:HALO_C, :] = jnp.zeros((HALO_C, D_C), f32)

    h = _rmsnorm(x_ref[...], g_ref[...]).astype(bf16)
    z_ref[HALO_C:HALO_C + TM_MIX, :] = _dot(h, wval_ref[...]) * jax.nn.sigmoid(_dot(h, wgate_ref[...]))
    u = _dot(h, wu_ref[...])
    v = _dot(h, wv_ref[...])
    b = _dot(h, wb_ref[...])
    c = _dot(h, wc_ref[...])
    u_ref[...] = _gelu(u)
    v_ref[...] = _gelu(v)
    b_ref[...] = b
    p_ref[HALO_B:HALO_B + TM_MIX, :] = c * _dot(h, wx_ref[...])
    last_tile = lambda r: _exact_zero(r[TM_MIX - SUBLANES:TM_MIX, D_BR - 128:D_BR])
    pace = [None] * 6 + [last_tile(u)] * 4 + [last_tile(v)] * 3 + [last_tile(b)] * 3
    _conf_conv(c_cw_ref, c_cb_ref, c_lng_ref, c_lnb_ref, yc_ref, z_ref, zs_ref, c_ref, pace)
    _sconv(b_cw_ref, yb_ref, p_ref, b_ref)
    _gmlp_spatial(u_ref, v_ref, a_lng_ref, a_lnb_ref, ws_ref, bst_ref, ya_ref)


def _mix(x, g, w_in, a_lng, a_lnb, ws, bst, b_cw, c_cw, c_cb, c_lng, c_lnb, l, seq_tiles):
    m = x.shape[0]
    tm = TM_MIX
    wspec = lambda c: _resident((D_MODEL, D_BR), lambda i: (0, c))
    vec = lambda d: pl.BlockSpec((None, 1, d), lambda i: (l, 0, 0))
    out = pl.BlockSpec((tm, D_BR), lambda i: (i, 0))
    return pl.pallas_call(
        functools.partial(_mix_kernel, seq_tiles=seq_tiles),
        grid=(m // tm,),
        in_specs=[
            pl.BlockSpec((tm, D_MODEL), lambda i: (i, 0)),
            vec(D_MODEL),
            wspec(0), wspec(1), wspec(2), wspec(3), wspec(4), wspec(5), wspec(6),
            vec(D_A), vec(D_A),
            pl.BlockSpec((None, A_GROUPS, GMLP_BLOCK, GMLP_BLOCK), lambda i: (l, 0, 0, 0)),
            pl.BlockSpec((None, GMLP_BLOCK, A_GROUPS), lambda i: (l, 0, 0)),
            pl.BlockSpec((None, CONV_B, D_B), lambda i: (l, 0, 0)),
            pl.BlockSpec((None, CONV_C, D_C), lambda i: (l, 0, 0)),
            vec(D_C), vec(D_C), vec(D_C),
        ],
        out_specs=[out, out, out],
        out_shape=[jax.ShapeDtypeStruct((m, D_BR), bf16)] * N_BRANCH,
        scratch_shapes=[
            pltpu.VMEM((HALO_B + tm, D_B), f32),
            pltpu.VMEM((tm, D_B), f32),
            pltpu.VMEM((tm, D_A), f32),
            pltpu.VMEM((tm, D_A), f32),
            pltpu.VMEM((HALO_C + tm, D_C), f32),
            pltpu.VMEM((SUBLANES - 1, SHIFT_ROWS, CONV_HALF), f32),
            pltpu.VMEM((tm, D_C), f32),
        ],
        compiler_params=_params("arbitrary"),
        name="mix",
    )(x, g, *([w_in] * 7), a_lng, a_lnb, ws, bst, b_cw, c_cw, c_cb, c_lng, c_lnb)


def _merge_kernel(*refs, n_cast):
    (x_ref, g_ref, ya_ref, yb_ref, yc_ref, wg0_ref, wg1_ref, wg2_ref, bg_ref,
     wb0_ref, wb1_ref, wb2_ref, wo_ref) = refs[:13]
    src_refs = refs[13:13 + n_cast]
    o_ref = refs[13 + n_cast]
    dst_refs = refs[14 + n_cast:14 + 2 * n_cast]
    h_ref = refs[14 + 2 * n_cast]
    n = pl.program_id(1)

    branches = ((ya_ref, wg0_ref, wb0_ref), (yb_ref, wg1_ref, wb1_ref), (yc_ref, wg2_ref, wb2_ref))

    def step(acc_ref):
        h = h_ref[...]
        merged = None
        for k, (y_ref, wg_ref, wb_ref) in enumerate(branches):
            gate = jax.nn.sigmoid(_dot(h, wg_ref[...]) + bg_ref[k:k + 1, :])
            term = gate * _dot(y_ref[...], wb_ref[...])
            merged = term if merged is None else merged + term
        o_ref[...] = acc_ref[...] + _dot(merged.astype(bf16), wo_ref[...])
        _run_casts(src_refs, dst_refs)

    @pl.when(n == 0)
    def _():
        h_ref[...] = _rmsnorm(x_ref[...], g_ref[...]).astype(bf16)
        step(x_ref)

    @pl.when(n > 0)
    def _():
        step(o_ref)


def _merge(x, g, ya, yb, yc, w_in, bg, wbr, wo, l, jobs):
    m = x.shape[0]
    tm = TM_MERGE
    row = lambda d: pl.BlockSpec((tm, d), lambda i, n: (i, 0))
    gspec = lambda k: pl.BlockSpec((D_MODEL, TN), lambda i, n: (0, (GATE_COL0 + k * D_MODEL) // TN + n))
    bspec = lambda k: pl.BlockSpec((D_BR, TN), lambda i, n: (k, n))
    cast_in, cast_out, cast_shapes = _cast_specs(jobs)
    return pl.pallas_call(
        functools.partial(_merge_kernel, n_cast=len(jobs)),
        grid=(m // tm, D_MODEL // TN),
        in_specs=[
            row(D_MODEL),
            pl.BlockSpec((None, 1, D_MODEL), lambda i, n: (l, 0, 0)),
            row(D_BR), row(D_BR), row(D_BR),
            gspec(0), gspec(1), gspec(2),
            pl.BlockSpec((None, N_BRANCH, TN), lambda i, n: (l, 0, n)),
            bspec(0), bspec(1), bspec(2),
            pl.BlockSpec((TN, D_MODEL), lambda i, n: (n, 0)),
        ] + cast_in,
        out_specs=[row(D_MODEL)] + cast_out,
        out_shape=[jax.ShapeDtypeStruct((m, D_MODEL), f32)] + cast_shapes,
        scratch_shapes=[pltpu.VMEM((tm, D_MODEL), bf16)],
        compiler_params=_params("arbitrary", "arbitrary"),
        name="merge",
    )(x, g, ya, yb, yc, w_in, w_in, w_in, bg, wbr, wbr, wbr, wo, *[job[0] for job in jobs])


def kernel(x, ffn1_norm, ffn1_w1, ffn1_w3, ffn1_w2, mix_norm, w_in, b_gate, gmlp_ln_g, gmlp_ln_b, gmlp_w_s, gmlp_b_s, sconv_w, conf_conv_w, conf_conv_b, conf_ln_g, conf_ln_b, w_branch, w_out, ffn2_norm, ffn2_w1, ffn2_w3, ffn2_w2, final_norm):
    bsz, s, d = x.shape
    depth = w_in.shape[0]
    m = bsz * s
    assert d == D_MODEL and s % TM_FFN == 0 and s % TM_MIX == 0 and s % TM_MERGE == 0
    seq_tiles = s // TM_MIX
    vec = lambda p: p[:, None, :]

    ffn1_g, ffn2_g, mix_g = vec(ffn1_norm), vec(ffn2_norm), vec(mix_norm)
    a_lng, a_lnb = vec(gmlp_ln_g), vec(gmlp_ln_b)
    c_b, c_lng, c_lnb = vec(conf_conv_b), vec(conf_ln_g), vec(conf_ln_b)
    bs_t = jnp.swapaxes(gmlp_b_s, 1, 2)
    gf = final_norm[None, :]
    w_branch2 = w_branch.reshape(depth, N_BRANCH * D_BR, D_MODEL)

    ni, nj = m // TM_FFN, D_FF // TF
    up_block, up_map = (D_MODEL // ni, TF), lambda i, j: (i, j)
    down_block, down_map = (TF, D_MODEL // ni), lambda i, j: (j, i)
    sq_cols = 8
    sq_map = lambda i, j: (i, jnp.minimum(j, sq_cols - 1))
    br_block = (N_BRANCH * D_BR // ni, D_MODEL // sq_cols)
    out_block = (D_MODEL // ni, D_MODEL // sq_cols)
    mi, mn = m // TM_MERGE, D_MODEL // TN
    mg_map = lambda i, n: (i, n)
    mg_down_block = (D_FF // mi, D_MODEL // mn)
    mg_in_block = (D_MODEL // mi, D_IN // mn)
    assert nj >= sq_cols and D_FF % mi == 0 and D_IN % (mn * 128) == 0

    f1 = tuple(w[0].astype(bf16) for w in (ffn1_w1, ffn1_w3, ffn1_w2))
    w_in_l = w_in[0].astype(bf16)

    xf = x.reshape(m, d)
    for l in range(depth):
        last = l == depth - 1
        jobs = [(ffn2_w1, l, up_block, up_map), (ffn2_w3, l, up_block, up_map),
                (w_branch2, l, br_block, sq_map), (w_out, l, out_block, sq_map)]
        xf, f2_w1, f2_w3, w_br_l, w_out_l = _ffn(xf, ffn1_g, *f1, gf, l, False, jobs)

        ya, yb, yc = _mix(xf, mix_g, w_in_l, a_lng, a_lnb, gmlp_w_s, bs_t, sconv_w,
                          conf_conv_w, c_b, c_lng, c_lnb, l, seq_tiles)

        jobs = [(ffn2_w2, l, mg_down_block, mg_map)]
        if not last:
            jobs.append((w_in, l + 1, mg_in_block, mg_map))
        xf, f2_w2, *nxt = _merge(xf, mix_g, ya, yb, yc, w_in_l, b_gate, w_br_l, w_out_l, l, jobs)
        if not last:
            w_in_l = nxt[0]

        jobs = [] if last else [(ffn1_w1, l + 1, up_block, up_map), (ffn1_w3, l + 1, up_block, up_map),
                                (ffn1_w2, l + 1, down_block, down_map)]
        xf, *f1 = _ffn(xf, ffn2_g, f2_w1, f2_w3, f2_w2, gf, l, last, jobs)
    return xf.reshape(bsz, s, d)
```

```python
import functools

import jax
import jax.numpy as jnp
from jax import lax
from jax.experimental import pallas as pl
from jax.experimental.pallas import tpu as pltpu

D_MODEL = 2048
CHUNK = 64
GMLP_BLOCK = 128
D_A = 1024
A_GROUPS = 8
D_B = 1024
CONV_B = 3
D_C = 1024
CONV_C = 31
N_BRANCH = 3
D_BR = 1024
D_FF = 5632
D_IN = 2 * D_A + 3 * D_B + 2 * D_C + N_BRANCH * D_MODEL
EPS = 1e-6
GATE_COL0 = 2 * D_A + 3 * D_B + 2 * D_C

SUBLANES = 8
V7X_VMEM_LIMIT = 60000 * 1024

TM_FFN = 1024
TF = 512
TM_MIX = 256
TM_MERGE = 512
TN = 512
HALO_B = SUBLANES
HALO_C = 4 * SUBLANES
SHIFT_ROWS = TM_MIX + HALO_C - SUBLANES
CONV_ROWS = 128
CONV_LANES = 128
CONV_HALF = D_C // 2

f32 = jnp.float32
bf16 = jnp.bfloat16


def _rmsnorm(x, g):
    return x * lax.rsqrt(jnp.mean(x * x, axis=-1, keepdims=True) + EPS) * g


def _layernorm(x, g, b):
    mu = jnp.mean(x, axis=-1, keepdims=True)
    xc = x - mu
    return xc * lax.rsqrt(jnp.mean(xc * xc, axis=-1, keepdims=True) + EPS) * g + b


def _dot(a, b):
    return jnp.dot(a, b, preferred_element_type=f32)


def _gelu(x):
    return 0.5 * x * (1.0 + lax.erf(x * (2.0 ** -0.5)))


def _params(*sem):
    return pltpu.CompilerParams(dimension_semantics=sem, vmem_limit_bytes=V7X_VMEM_LIMIT)


def _resident(shape, index_map):
    return pl.BlockSpec(shape, index_map, pipeline_mode=pl.Buffered(1))


def _cast_specs(jobs):
    in_specs, out_specs, out_shapes = [], [], []
    for w, layer, block, index_map in jobs:
        in_specs.append(pl.BlockSpec(
            (None,) + block, lambda *ids, layer=layer, index_map=index_map: (layer,) + index_map(*ids)))
        out_specs.append(pl.BlockSpec(block, index_map))
        out_shapes.append(jax.ShapeDtypeStruct(w.shape[1:], bf16))
    return in_specs, out_specs, out_shapes


def _run_casts(src_refs, dst_refs):
    for src, dst in zip(src_refs, dst_refs):
        dst[...] = src[...].astype(bf16)


def _ffn_kernel(*refs, final, n_cast):
    x_ref, g_ref, w1_ref, w3_ref, w2_ref, gf_ref = refs[:6]
    src_refs = refs[6:6 + n_cast]
    o_ref = refs[6 + n_cast]
    dst_refs = refs[7 + n_cast:7 + 2 * n_cast]
    h_ref = refs[7 + 2 * n_cast]
    j = pl.program_id(1)

    def step(acc_ref):
        h = h_ref[...]
        a = _dot(h, w1_ref[...])
        b = _dot(h, w3_ref[...])
        act = (0.5 * (a * jax.nn.sigmoid(a)) * b).astype(bf16)
        o_ref[...] = acc_ref[...] + _dot(act, w2_ref[...])
        _run_casts(src_refs, dst_refs)

    @pl.when(j == 0)
    def _():
        h_ref[...] = _rmsnorm(x_ref[...], g_ref[...]).astype(bf16)
        step(x_ref)

    @pl.when(j > 0)
    def _():
        step(o_ref)

    if final:
        @pl.when(j == pl.num_programs(1) - 1)
        def _():
            o_ref[...] = _rmsnorm(o_ref[...], gf_ref[...])


def _ffn(x, g, w1, w3, w2, gf, l, final, jobs):
    m = x.shape[0]
    tm = TM_FFN
    cast_in, cast_out, cast_shapes = _cast_specs(jobs)
    return pl.pallas_call(
        functools.partial(_ffn_kernel, final=final, n_cast=len(jobs)),
        grid=(m // tm, D_FF // TF),
        in_specs=[
            pl.BlockSpec((tm, D_MODEL), lambda i, j: (i, 0)),
            pl.BlockSpec((None, 1, D_MODEL), lambda i, j: (l, 0, 0)),
            pl.BlockSpec((D_MODEL, TF), lambda i, j: (0, j)),
            pl.BlockSpec((D_MODEL, TF), lambda i, j: (0, j)),
            pl.BlockSpec((TF, D_MODEL), lambda i, j: (j, 0)),
            pl.BlockSpec((1, D_MODEL), lambda i, j: (0, 0)),
        ] + cast_in,
        out_specs=[pl.BlockSpec((tm, D_MODEL), lambda i, j: (i, 0))] + cast_out,
        out_shape=[jax.ShapeDtypeStruct((m, D_MODEL), f32)] + cast_shapes,
        scratch_shapes=[pltpu.VMEM((tm, D_MODEL), bf16)],
        compiler_params=_params("arbitrary", "arbitrary"),
        name="ffn",
    )(x, g, w1, w3, w2, gf, *[job[0] for job in jobs])


def _gmlp_spatial(u_ref, v_ref, lng_ref, lnb_ref, ws_ref, bst_ref, o_ref):
    v = _layernorm(v_ref[...], lng_ref[...], lnb_ref[...]).astype(bf16)
    row_chunk = lax.broadcasted_iota(jnp.int32, (GMLP_BLOCK, GMLP_BLOCK), 0) // CHUNK
    col_chunk = lax.broadcasted_iota(jnp.int32, (GMLP_BLOCK, GMLP_BLOCK), 1) // CHUNK
    mask = col_chunk <= row_chunk
    gw = D_A // A_GROUPS
    blocks = [slice(n * GMLP_BLOCK, (n + 1) * GMLP_BLOCK) for n in range(TM_MIX // GMLP_BLOCK)]
    for g in range(A_GROUPS):
        ws = jnp.where(mask, ws_ref[g], 0.0).astype(bf16)
        bias = bst_ref[:, g:g + 1]
        cols = slice(g * gw, (g + 1) * gw)
        mixed = _dot(ws, jnp.concatenate([v[rows, cols] for rows in blocks], axis=1)) + bias
        for n, rows in enumerate(blocks):
            o_ref[rows, cols] = (u_ref[rows, cols] * mixed[:, n * gw:(n + 1) * gw]).astype(bf16)


def _sconv(cw_ref, o_ref, p_ref, b_ref):
    for r in range(0, TM_MIX, CONV_ROWS):
        for c in range(0, D_B, CONV_LANES):
            lanes = slice(c, c + CONV_LANES)
            acc = None
            for k in range(CONV_B):
                start = HALO_B - (CONV_B - 1) + k + r
                term = p_ref[start:start + CONV_ROWS, lanes] * cw_ref[k:k + 1, lanes]
                acc = term if acc is None else acc + term
            o_ref[r:r + CONV_ROWS, lanes] = (b_ref[r:r + CONV_ROWS, lanes] * acc).astype(bf16)
    p_ref[0:HALO_B, :] = p_ref[TM_MIX:TM_MIX + HALO_B, :]


def _conf_conv(cw_ref, cb_ref, lng_ref, lnb_ref, o_ref, z_ref, zs_ref, c_ref):
    first = HALO_C - (CONV_C - 1)
    for half in range(0, D_C, CONV_HALF):
        for a in range(1, SUBLANES):
            zs_ref[a - 1] = z_ref[a:a + SHIFT_ROWS, half:half + CONV_HALF]
        for r in range(0, TM_MIX, CONV_ROWS):
            for c in range(0, CONV_HALF, CONV_LANES):
                acc = None
                for k in range(CONV_C):
                    q, a = divmod(first + k, SUBLANES)
                    row0 = q * SUBLANES + r
                    if a == 0:
                        src = z_ref[row0:row0 + CONV_ROWS, half + c:half + c + CONV_LANES]
                    else:
                        src = zs_ref[a - 1, row0:row0 + CONV_ROWS, c:c + CONV_LANES]
                    term = src * cw_ref[k:k + 1, half + c:half + c + CONV_LANES]
                    acc = term if acc is None else acc + term
                c_ref[r:r + CONV_ROWS, half + c:half + c + CONV_LANES] = acc
    z_ref[0:HALO_C, :] = z_ref[TM_MIX:TM_MIX + HALO_C, :]
    y = _layernorm(c_ref[...] + cb_ref[...], lng_ref[...], lnb_ref[...])
    o_ref[...] = (y * jax.nn.sigmoid(y)).astype(bf16)


def _mix_kernel(x_ref, g_ref, wu_ref, wv_ref, wb_ref, wc_ref, wx_ref, wval_ref, wgate_ref,
                a_lng_ref, a_lnb_ref, ws_ref, bst_ref, b_cw_ref,
                c_cw_ref, c_cb_ref, c_lng_ref, c_lnb_ref,
                ya_ref, yb_ref, yc_ref,
                p_ref, b_ref, u_ref, v_ref, z_ref, zs_ref, c_ref, *, seq_tiles):
    @pl.when(pl.program_id(0) % seq_tiles == 0)
    def _():
        p_ref[0:HALO_B, :] = jnp.zeros((HALO_B, D_B), f32)
        z_ref[0:HALO_C, :] = jnp.zeros((HALO_C, D_C), f32)

    h = _rmsnorm(x_ref[...], g_ref[...]).astype(bf16)
    z_ref[HALO_C:HALO_C + TM_MIX, :] = _dot(h, wval_ref[...]) * jax.nn.sigmoid(_dot(h, wgate_ref[...]))
    _conf_conv(c_cw_ref, c_cb_ref, c_lng_ref, c_lnb_ref, yc_ref, z_ref, zs_ref, c_ref)
    u_ref[...] = _gelu(_dot(h, wu_ref[...]))
    v_ref[...] = _gelu(_dot(h, wv_ref[...]))
    _gmlp_spatial(u_ref, v_ref, a_lng_ref, a_lnb_ref, ws_ref, bst_ref, ya_ref)
    b_ref[...] = _dot(h, wb_ref[...])
    p_ref[HALO_B:HALO_B + TM_MIX, :] = _dot(h, wc_ref[...]) * _dot(h, wx_ref[...])
    _sconv(b_cw_ref, yb_ref, p_ref, b_ref)


def _mix(x, g, w_in, a_lng, a_lnb, ws, bst, b_cw, c_cw, c_cb, c_lng, c_lnb, l, seq_tiles):
    m = x.shape[0]
    tm = TM_MIX
    wspec = lambda c: _resident((D_MODEL, D_BR), lambda i: (0, c))
    vec = lambda d: pl.BlockSpec((None, 1, d), lambda i: (l, 0, 0))
    out = pl.BlockSpec((tm, D_BR), lambda i: (i, 0))
    return pl.pallas_call(
        functools.partial(_mix_kernel, seq_tiles=seq_tiles),
        grid=(m // tm,),
        in_specs=[
            pl.BlockSpec((tm, D_MODEL), lambda i: (i, 0)),
            vec(D_MODEL),
            wspec(0), wspec(1), wspec(2), wspec(3), wspec(4), wspec(5), wspec(6),
            vec(D_A), vec(D_A),
            pl.BlockSpec((None, A_GROUPS, GMLP_BLOCK, GMLP_BLOCK), lambda i: (l, 0, 0, 0)),
            pl.BlockSpec((None, GMLP_BLOCK, A_GROUPS), lambda i: (l, 0, 0)),
            pl.BlockSpec((None, CONV_B, D_B), lambda i: (l, 0, 0)),
            pl.BlockSpec((None, CONV_C, D_C), lambda i: (l, 0, 0)),
            vec(D_C), vec(D_C), vec(D_C),
        ],
        out_specs=[out, out, out],
        out_shape=[jax.ShapeDtypeStruct((m, D_BR), bf16)] * N_BRANCH,
        scratch_shapes=[
            pltpu.VMEM((HALO_B + tm, D_B), f32),
            pltpu.VMEM((tm, D_B), f32),
            pltpu.VMEM((tm, D_A), f32),
            pltpu.VMEM((tm, D_A), f32),
            pltpu.VMEM((HALO_C + tm, D_C), f32),
            pltpu.VMEM((SUBLANES - 1, SHIFT_ROWS, CONV_HALF), f32),
            pltpu.VMEM((tm, D_C), f32),
        ],
        compiler_params=_params("arbitrary"),
        name="mix",
    )(x, g, *([w_in] * 7), a_lng, a_lnb, ws, bst, b_cw, c_cw, c_cb, c_lng, c_lnb)


def _merge_kernel(*refs, n_cast):
    (x_ref, g_ref, ya_ref, yb_ref, yc_ref, wg0_ref, wg1_ref, wg2_ref, bg_ref,
     wb0_ref, wb1_ref, wb2_ref, wo_ref) = refs[:13]
    src_refs = refs[13:13 + n_cast]
    o_ref = refs[13 + n_cast]
    dst_refs = refs[14 + n_cast:14 + 2 * n_cast]
    h_ref = refs[14 + 2 * n_cast]
    n = pl.program_id(1)

    branches = ((ya_ref, wg0_ref, wb0_ref), (yb_ref, wg1_ref, wb1_ref), (yc_ref, wg2_ref, wb2_ref))

    def step(acc_ref):
        h = h_ref[...]
        merged = None
        for k, (y_ref, wg_ref, wb_ref) in enumerate(branches):
            gate = jax.nn.sigmoid(_dot(h, wg_ref[...]) + bg_ref[k:k + 1, :])
            term = gate * _dot(y_ref[...], wb_ref[...])
            merged = term if merged is None else merged + term
        o_ref[...] = acc_ref[...] + _dot(merged.astype(bf16), wo_ref[...])
        _run_casts(src_refs, dst_refs)

    @pl.when(n == 0)
    def _():
        h_ref[...] = _rmsnorm(x_ref[...], g_ref[...]).astype(bf16)
        step(x_ref)

    @pl.when(n > 0)
    def _():
        step(o_ref)


def _merge(x, g, ya, yb, yc, w_in, bg, wbr, wo, l, jobs):
    m = x.shape[0]
    tm = TM_MERGE
    row = lambda d: pl.BlockSpec((tm, d), lambda i, n: (i, 0))
    gspec = lambda k: pl.BlockSpec((D_MODEL, TN), lambda i, n: (0, (GATE_COL0 + k * D_MODEL) // TN + n))
    bspec = lambda k: pl.BlockSpec((D_BR, TN), lambda i, n: (k, n))
    cast_in, cast_out, cast_shapes = _cast_specs(jobs)
    return pl.pallas_call(
        functools.partial(_merge_kernel, n_cast=len(jobs)),
        grid=(m // tm, D_MODEL // TN),
        in_specs=[
            row(D_MODEL),
            pl.BlockSpec((None, 1, D_MODEL), lambda i, n: (l, 0, 0)),
            row(D_BR), row(D_BR), row(D_BR),
            gspec(0), gspec(1), gspec(2),
            pl.BlockSpec((None, N_BRANCH, TN), lambda i, n: (l, 0, n)),
            bspec(0), bspec(1), bspec(2),
            pl.BlockSpec((TN, D_MODEL), lambda i, n: (n, 0)),
        ] + cast_in,
        out_specs=[row(D_MODEL)] + cast_out,
        out_shape=[jax.ShapeDtypeStruct((m, D_MODEL), f32)] + cast_shapes,
        scratch_shapes=[pltpu.VMEM((tm, D_MODEL), bf16)],
        compiler_params=_params("arbitrary", "arbitrary"),
        name="merge",
    )(x, g, ya, yb, yc, w_in, w_in, w_in, bg, wbr, wbr, wbr, wo, *[job[0] for job in jobs])


def kernel(x, ffn1_norm, ffn1_w1, ffn1_w3, ffn1_w2, mix_norm, w_in, b_gate, gmlp_ln_g, gmlp_ln_b, gmlp_w_s, gmlp_b_s, sconv_w, conf_conv_w, conf_conv_b, conf_ln_g, conf_ln_b, w_branch, w_out, ffn2_norm, ffn2_w1, ffn2_w3, ffn2_w2, final_norm):
    bsz, s, d = x.shape
    depth = w_in.shape[0]
    m = bsz * s
    assert d == D_MODEL and s % TM_FFN == 0 and s % TM_MIX == 0 and s % TM_MERGE == 0
    seq_tiles = s // TM_MIX
    vec = lambda p: p[:, None, :]

    ffn1_g, ffn2_g, mix_g = vec(ffn1_norm), vec(ffn2_norm), vec(mix_norm)
    a_lng, a_lnb = vec(gmlp_ln_g), vec(gmlp_ln_b)
    c_b, c_lng, c_lnb = vec(conf_conv_b), vec(conf_ln_g), vec(conf_ln_b)
    bs_t = jnp.swapaxes(gmlp_b_s, 1, 2)
    gf = final_norm[None, :]
    w_branch2 = w_branch.reshape(depth, N_BRANCH * D_BR, D_MODEL)

    ni, nj = m // TM_FFN, D_FF // TF
    up_block, up_map = (D_MODEL // ni, TF), lambda i, j: (i, j)
    down_block, down_map = (TF, D_MODEL // ni), lambda i, j: (j, i)
    sq_cols = 8
    sq_map = lambda i, j: (i, jnp.minimum(j, sq_cols - 1))
    br_block = (N_BRANCH * D_BR // ni, D_MODEL // sq_cols)
    out_block = (D_MODEL // ni, D_MODEL // sq_cols)
    mi, mn = m // TM_MERGE, D_MODEL // TN
    mg_map = lambda i, n: (i, n)
    mg_down_block = (D_FF // mi, D_MODEL // mn)
    mg_in_block = (D_MODEL // mi, D_IN // mn)
    assert nj >= sq_cols and D_FF % mi == 0 and D_IN % (mn * 128) == 0

    f1 = tuple(w[0].astype(bf16) for w in (ffn1_w1, ffn1_w3, ffn1_w2))
    w_in_l = w_in[0].astype(bf16)

    xf = x.reshape(m, d)
    for l in range(depth):
        last = l == depth - 1
        jobs = [(ffn2_w1, l, up_block, up_map), (ffn2_w3, l, up_block, up_map),
                (w_branch2, l, br_block, sq_map), (w_out, l, out_block, sq_map)]
        xf, f2_w1, f2_w3, w_br_l, w_out_l = _ffn(xf, ffn1_g, *f1, gf, l, False, jobs)

        ya, yb, yc = _mix(xf, mix_g, w_in_l, a_lng, a_lnb, gmlp_w_s, bs_t, sconv_w,
                          conf_conv_w, c_b, c_lng, c_lnb, l, seq_tiles)

        jobs = [(ffn2_w2, l, mg_down_block, mg_map)]
        if not last:
            jobs.append((w_in, l + 1, mg_in_block, mg_map))
        xf, f2_w2, *nxt = _merge(xf, mix_g, ya, yb, yc, w_in_l, b_gate, w_br_l, w_out_l, l, jobs)
        if not last:
            w_in_l = nxt[0]

        jobs = [] if last else [(ffn1_w1, l + 1, up_block, up_map), (ffn1_w3, l + 1, up_block, up_map),
                                (ffn1_w2, l + 1, down_block, down_map)]
        xf, *f1 = _ffn(xf, ffn2_g, f2_w1, f2_w3, f2_w2, gf, l, last, jobs)
    return xf.reshape(bsz, s, d)
```
